```python
import math
import jax, jax.numpy as jnp
from jax import lax
import numpy as np

D_MODEL = 1024
BATCH = 8
SEQ = 2048
DEPTH = 1
DEC_BATCH = 128
DEC_SEQ = 1
PAST_LEN = 16384
PAGE_SIZE = 128

N_META = 16
S5_WIDTH = D_MODEL // 2
LRU_WIDTH = D_MODEL - S5_WIDTH
MIX_WIDTH = S5_WIDTH + LRU_WIDTH
S5_GROUP = 16
S5_GROUPS = S5_WIDTH // S5_GROUP
S5_STATE = 64
S5_STEP_MIN = 1e-3
S5_STEP_MAX = 1e-1
LRU_HEAD_DIM = 64
LRU_HEADS = LRU_WIDTH // LRU_HEAD_DIM
CONV_WIDTH = 4
LRU_C = 8.0
N_EXPERTS = 32
TOP_K = 4
D_FF = D_MODEL
SWIGLU_ALPHA = 1.702
SWIGLU_LIMIT = 7.0
MOE_BLOCK = 128
EPS = 1e-6

kernel_name = 'hymba_s5_rglru_moe_step'


def rms_norm(x, g):
    x = x.astype(jnp.float32)
    return x * lax.rsqrt(jnp.mean(x * x, axis=-1, keepdims=True) + EPS) * g


def linear_scan(a, b, h0):
    def comb(e1, e2):
        a1, b1 = e1
        a2, b2 = e2
        return a1 * a2, a2 * b1 + b2
    a_cum, b_cum = lax.associative_scan(comb, (a, b), axis=1)
    return a_cum * h0[:, None] + b_cum


def complex_scan(a_re, a_im, b_re, b_im, h0_re, h0_im):
    def comb(e1, e2):
        ar1, ai1, br1, bi1 = e1
        ar2, ai2, br2, bi2 = e2
        return (ar2 * ar1 - ai2 * ai1, ar2 * ai1 + ai2 * ar1,
                ar2 * br1 - ai2 * bi1 + br2, ar2 * bi1 + ai2 * br1 + bi2)
    ar, ai, br, bi = lax.associative_scan(comb, (a_re, a_im, b_re, b_im), axis=1)
    h0r = h0_re[:, None]
    h0i = h0_im[:, None]
    return ar * h0r - ai * h0i + br, ar * h0i + ai * h0r + bi


def s5_mixer(u, h0_re, h0_im, lam_re, lam_im, log_step, b_re, b_im, c_re, c_im, d, w_glu, b_glu):
    nb, L, _ = u.shape
    step = jnp.exp(log_step)[:, None]
    dr = lam_re * step
    di = lam_im * step
    mag = jnp.exp(dr)
    lb_re = mag * jnp.cos(di)
    lb_im = mag * jnp.sin(di)
    den = lam_re * lam_re + lam_im * lam_im
    num_re = lb_re - 1.0
    coef_re = (num_re * lam_re + lb_im * lam_im) / den
    coef_im = (lb_im * lam_re - num_re * lam_im) / den
    bb_re = coef_re[..., None] * b_re - coef_im[..., None] * b_im
    bb_im = coef_re[..., None] * b_im + coef_im[..., None] * b_re
    ug = u.reshape(nb, L, S5_GROUPS, S5_GROUP)
    bu_re = jnp.einsum('blgh,gph->blgp', ug, bb_re)
    bu_im = jnp.einsum('blgh,gph->blgp', ug, bb_im)
    a_re = jnp.broadcast_to(lb_re[None, None], (1, L, S5_GROUPS, S5_STATE))
    a_im = jnp.broadcast_to(lb_im[None, None], (1, L, S5_GROUPS, S5_STATE))
    h_re, h_im = complex_scan(a_re, a_im, bu_re, bu_im, h0_re, h0_im)
    y = jnp.einsum('blgp,ghp->blgh', h_re, c_re) - jnp.einsum('blgp,ghp->blgh', h_im, c_im)
    y = y.reshape(nb, L, S5_WIDTH) + d * u
    g = jax.nn.gelu(y)
    out = g * jax.nn.sigmoid(g @ w_glu + b_glu)
    return out, h_re[:, -1], h_im[:, -1]


def rglru_mixer(xb, gate, h0, conv_buf, conv_w, conv_b, w_a, b_a, w_x, b_x, lam, reset_first):
    nb, L, W = xb.shape
    xp = jnp.concatenate([conv_buf, xb], axis=1)
    xc = conv_b
    for k in range(CONV_WIDTH):
        xc = xc + conv_w[k] * xp[:, k:k + L]
    new_buf = xp[:, L:]
    xh = xc.reshape(nb, L, LRU_HEADS, LRU_HEAD_DIM)
    r = jax.nn.sigmoid(jnp.einsum('blhi,hij->blhj', xh, w_a).reshape(nb, L, W) + b_a)
    i = jax.nn.sigmoid(jnp.einsum('blhi,hij->blhj', xh, w_x).reshape(nb, L, W) + b_x)
    log_a = LRU_C * r * jax.nn.log_sigmoid(lam)
    a = jnp.exp(log_a)
    mult = jnp.sqrt(-jnp.expm1(2.0 * log_a))
    if reset_first:
        mult = mult.at[:, 0].set(1.0)
    h = linear_scan(a, mult * (i * xc), h0)
    out = h * jax.nn.gelu(gate)
    return out, h[:, -1], new_buf


def mixer_sublayer(x, norm_g, w_in, lam_re, lam_im, log_step, b_re, b_im, c_re, c_im, d, w_glu, b_glu,
                   conv_w, conv_b, w_a, b_a, w_x, b_x, lam, g_s5, g_lru, w_out,
                   s5_h0_re, s5_h0_im, lru_h0, conv_buf, reset_first):
    proj = rms_norm(x, norm_g) @ w_in
    u = proj[..., :S5_WIDTH]
    xb = proj[..., S5_WIDTH:S5_WIDTH + LRU_WIDTH]
    gate = proj[..., S5_WIDTH + LRU_WIDTH:]
    y_a, s5_re, s5_im = s5_mixer(u, s5_h0_re, s5_h0_im, lam_re, lam_im, log_step,
                                 b_re, b_im, c_re, c_im, d, w_glu, b_glu)
    y_b, lru_h, new_buf = rglru_mixer(xb, gate, lru_h0, conv_buf, conv_w, conv_b,
                                      w_a, b_a, w_x, b_x, lam, reset_first)
    mixed = jnp.concatenate([rms_norm(y_a, g_s5), rms_norm(y_b, g_lru)], axis=-1) @ w_out
    return x + mixed, s5_re, s5_im, lru_h, new_buf


def moe(x, w_router, b_router, w_up, b_up, w_down, b_down):
    T, D = x.shape
    logits = x @ w_router + b_router
    top_v, top_i = lax.top_k(logits, TOP_K)
    gates = jax.nn.softmax(top_v, axis=-1)
    flat_e = top_i.reshape(-1)
    flat_tok = jnp.repeat(jnp.arange(T, dtype=jnp.int32), TOP_K)
    order = jnp.argsort(flat_e)
    se = flat_e[order]
    stok = flat_tok[order]
    sg = gates.reshape(-1)[order]
    counts = jnp.bincount(flat_e, length=N_EXPERTS)
    padded = (counts + MOE_BLOCK - 1) // MOE_BLOCK * MOE_BLOCK
    start = jnp.cumsum(counts) - counts
    ends = jnp.cumsum(padded)
    pstart = ends - padded
    dest = pstart[se] + (jnp.arange(T * TOP_K) - start[se])
    n_blocks = -(-(T * TOP_K) // MOE_BLOCK) + N_EXPERTS
    n_rows = n_blocks * MOE_BLOCK
    row_tok = jnp.full((n_rows,), T, jnp.int32).at[dest].set(stok)
    row_gate = jnp.zeros((n_rows,), x.dtype).at[dest].set(sg)
    block_e = jnp.minimum(jnp.searchsorted(ends, jnp.arange(n_blocks) * MOE_BLOCK, side='right'), N_EXPERTS - 1)
    x_pad = jnp.concatenate([x, jnp.zeros((1, D), x.dtype)], axis=0)
    xs = x_pad[row_tok].reshape(n_blocks, MOE_BLOCK, D)

    def expert_block(args):
        xb, e = args
        hu = xb @ w_up[e] + b_up[e]
        glu = jnp.minimum(hu[:, :D_FF], SWIGLU_LIMIT)
        lin = jnp.clip(hu[:, D_FF:], -SWIGLU_LIMIT, SWIGLU_LIMIT)
        act = glu * jax.nn.sigmoid(SWIGLU_ALPHA * glu) * (lin + 1.0)
        return act @ w_down[e] + b_down[e]

    ys = lax.map(expert_block, (xs, block_e)).reshape(n_rows, D)
    return jax.ops.segment_sum(ys * row_gate[:, None], row_tok, num_segments=T + 1)[:T]


def setup_inputs(seed: int = 0) -> dict:
    key = jax.random.key(seed)
    k = jax.random.split(key, 40)
    f32 = jnp.float32

    def nrm(i, shape, s):
        return s * jax.random.normal(k[i], shape, f32)

    lam_im = jnp.broadcast_to(jnp.pi * jnp.arange(S5_STATE, dtype=f32), (DEPTH, S5_GROUPS, S5_STATE))
    u_lru = jax.random.uniform(k[25], (DEPTH, LRU_WIDTH), f32, minval=0.9, maxval=0.999)
    s_lru = u_lru ** (1.0 / LRU_C)
    return {
        'x_prompt': nrm(0, (BATCH, SEQ, D_MODEL), 1.0),
        'x_sample': nrm(1, (DEC_BATCH, DEC_SEQ, D_MODEL), 1.0),
        'state_s5_re': nrm(2, (DEPTH, DEC_BATCH, S5_GROUPS, S5_STATE), 0.5),
        'state_s5_im': nrm(3, (DEPTH, DEC_BATCH, S5_GROUPS, S5_STATE), 0.5),
        'state_lru_h': nrm(4, (DEPTH, DEC_BATCH, LRU_WIDTH), 0.5),
        'state_conv': nrm(5, (DEPTH, DEC_BATCH, CONV_WIDTH - 1, LRU_WIDTH), 1.0),
        'meta_tokens': nrm(6, (N_META, D_MODEL), 1.0),
        'norm_mix_g': 1.0 + nrm(7, (DEPTH, D_MODEL), 0.01),
        'w_in': nrm(8, (DEPTH, D_MODEL, S5_WIDTH + 2 * LRU_WIDTH), D_MODEL ** -0.5),
        's5_lambda_re': -0.5 + nrm(9, (DEPTH, S5_GROUPS, S5_STATE), 0.01),
        's5_lambda_im': lam_im + nrm(10, (DEPTH, S5_GROUPS, S5_STATE), 0.01),
        's5_log_step': jax.random.uniform(k[11], (DEPTH, S5_GROUPS), f32, minval=math.log(S5_STEP_MIN), maxval=math.log(S5_STEP_MAX)),
        's5_b_re': nrm(12, (DEPTH, S5_GROUPS, S5_STATE, S5_GROUP), (2.0 * S5_GROUP) ** -0.5),
        's5_b_im': nrm(13, (DEPTH, S5_GROUPS, S5_STATE, S5_GROUP), (2.0 * S5_GROUP) ** -0.5),
        's5_c_re': nrm(14, (DEPTH, S5_GROUPS, S5_GROUP, S5_STATE), (2.0 * S5_STATE) ** -0.5),
        's5_c_im': nrm(15, (DEPTH, S5_GROUPS, S5_GROUP, S5_STATE), (2.0 * S5_STATE) ** -0.5),
        's5_d': nrm(16, (DEPTH, S5_WIDTH), 1.0),
        's5_w_glu': nrm(17, (DEPTH, S5_WIDTH, S5_WIDTH), S5_WIDTH ** -0.5),
        's5_b_glu': nrm(18, (DEPTH, S5_WIDTH), 0.01),
        'conv_w': nrm(19, (DEPTH, CONV_WIDTH, LRU_WIDTH), CONV_WIDTH ** -0.5),
        'conv_b': nrm(20, (DEPTH, LRU_WIDTH), 0.01),
        'lru_w_a': nrm(21, (DEPTH, LRU_HEADS, LRU_HEAD_DIM, LRU_HEAD_DIM), LRU_HEAD_DIM ** -0.5),
        'lru_b_a': nrm(22, (DEPTH, LRU_WIDTH), 0.01),
        'lru_w_x': nrm(23, (DEPTH, LRU_HEADS, LRU_HEAD_DIM, LRU_HEAD_DIM), LRU_HEAD_DIM ** -0.5),
        'lru_b_x': nrm(24, (DEPTH, LRU_WIDTH), 0.01),
        'lru_lambda': jnp.log(s_lru) - jnp.log1p(-s_lru),
        'norm_s5_g': 1.0 + nrm(26, (DEPTH, S5_WIDTH), 0.01),
        'norm_lru_g': 1.0 + nrm(27, (DEPTH, LRU_WIDTH), 0.01),
        'w_out': nrm(28, (DEPTH, MIX_WIDTH, D_MODEL), MIX_WIDTH ** -0.5),
        'norm_ffn_g': 1.0 + nrm(29, (DEPTH, D_MODEL), 0.01),
        'w_router': nrm(30, (DEPTH, D_MODEL, N_EXPERTS), D_MODEL ** -0.5),
        'b_router': nrm(31, (DEPTH, N_EXPERTS), 0.01),
        'w_up': nrm(32, (DEPTH, N_EXPERTS, D_MODEL, 2 * D_FF), D_MODEL ** -0.5),
        'b_up': nrm(33, (DEPTH, N_EXPERTS, 2 * D_FF), 0.01),
        'w_down': nrm(34, (DEPTH, N_EXPERTS, D_FF, D_MODEL), D_FF ** -0.5),
        'b_down': nrm(35, (DEPTH, N_EXPERTS, D_MODEL), 0.01),
        'norm_f_g': 1.0 + nrm(36, (D_MODEL,), 0.01),
    }


def reference(x_prompt, x_sample, state_s5_re, state_s5_im, state_lru_h, state_conv, meta_tokens,
              norm_mix_g, w_in, s5_lambda_re, s5_lambda_im, s5_log_step, s5_b_re, s5_b_im, s5_c_re, s5_c_im,
              s5_d, s5_w_glu, s5_b_glu, conv_w, conv_b, lru_w_a, lru_b_a, lru_w_x, lru_b_x, lru_lambda,
              norm_s5_g, norm_lru_g, w_out, norm_ffn_g, w_router, b_router, w_up, b_up, w_down, b_down, norm_f_g):
    f32 = jnp.float32
    out_dtype = x_prompt.dtype
    bp = x_prompt.shape[0]
    bs = x_sample.shape[0]
    meta = jnp.broadcast_to(meta_tokens.astype(f32)[None], (bp, N_META, D_MODEL))
    hp = jnp.concatenate([meta, x_prompt.astype(f32)], axis=1)
    hs = x_sample.astype(f32)
    p_s5re, p_s5im, p_lru, p_conv = [], [], [], []
    s_s5re, s_s5im, s_lru, s_conv = [], [], [], []
    for l in range(DEPTH):
        lp = [a[l].astype(f32) for a in (norm_mix_g, w_in, s5_lambda_re, s5_lambda_im, s5_log_step,
                                          s5_b_re, s5_b_im, s5_c_re, s5_c_im, s5_d, s5_w_glu, s5_b_glu,
                                          conv_w, conv_b, lru_w_a, lru_b_a, lru_w_x, lru_b_x, lru_lambda,
                                          norm_s5_g, norm_lru_g, w_out)]
        hp, sr, si, lh, cb = mixer_sublayer(
            hp, *lp,
            jnp.zeros((bp, S5_GROUPS, S5_STATE), f32), jnp.zeros((bp, S5_GROUPS, S5_STATE), f32),
            jnp.zeros((bp, LRU_WIDTH), f32), jnp.zeros((bp, CONV_WIDTH - 1, LRU_WIDTH), f32), True)
        p_s5re.append(sr); p_s5im.append(si); p_lru.append(lh); p_conv.append(cb)
        hs, sr, si, lh, cb = mixer_sublayer(
            hs, *lp,
            state_s5_re[l].astype(f32), state_s5_im[l].astype(f32),
            state_lru_h[l].astype(f32), state_conv[l].astype(f32), False)
        s_s5re.append(sr); s_s5im.append(si); s_lru.append(lh); s_conv.append(cb)
        if l == DEPTH - 1:
            hp = hp[:, N_META:]
        mp = [a[l].astype(f32) for a in (w_router, b_router, w_up, b_up, w_down, b_down)]
        g_ffn = norm_ffn_g[l].astype(f32)
        hp = hp + moe(rms_norm(hp, g_ffn).reshape(-1, D_MODEL), *mp).reshape(hp.shape)
        hs = hs + moe(rms_norm(hs, g_ffn).reshape(-1, D_MODEL), *mp).reshape(hs.shape)
    g_f = norm_f_g.astype(f32)
    y_prompt = rms_norm(hp, g_f).astype(out_dtype)
    y_sample = rms_norm(hs, g_f).astype(out_dtype)
    return (y_prompt, y_sample,
            jnp.stack(p_s5re), jnp.stack(p_s5im), jnp.stack(p_lru), jnp.stack(p_conv),
            jnp.stack(s_s5re), jnp.stack(s_s5im), jnp.stack(s_lru), jnp.stack(s_conv))
```

```python
import functools

import jax
import jax.numpy as jnp
from jax import lax
from jax.experimental import pallas as pl
from jax.experimental.pallas import tpu as pltpu

D_MODEL = 1024
N_META = 16
S5_WIDTH = 512
LRU_WIDTH = 512
S5_GROUP = 16
S5_GROUPS = 32
S5_STATE = 64
S5_CH = S5_GROUPS * S5_STATE
CONV_WIDTH = 4
LRU_C = 8.0
N_EXPERTS = 32
TOP_K = 4
D_FF = 1024
SWIGLU_ALPHA = 1.702
SWIGLU_LIMIT = 7.0
EPS = 1e-6

SUBLANES = 8
LANES = 128
MXU_DIM = 256
VMEM_LIMIT_BYTES = 56 * 1024 * 1024

PROMPT_STEPS = 64
MOE_TILE = 256
ROUTE_CHUNK = 256
TOKEN_TILE = 128

F32 = jnp.float32
BF16 = jnp.bfloat16


def _rms(x, g):
    return x * lax.rsqrt(jnp.mean(x * x, axis=-1, keepdims=True) + EPS) * g


def _dot(a, b):
    return jnp.dot(a, b, preferred_element_type=F32)


def _mixer_rows(x, first, w, state, scratch, outs, *, nb, nq, rows, scan_cols):
    (gmix_ref, win_ref, wbu_ref, are_ref, aim_ref, wc_ref, d_ref, wglu_ref, bglu_ref,
     convw_ref, convb_ref, wax_ref, ba_ref, bx_ref, lsl_ref, gs5_ref, glru_ref, wout_ref,
     gffn_ref, wr_ref, br_ref) = w
    s5st, lrust, xext = state
    hbuf, abuf, bbuf = scratch
    hmid_ref, xn_ref, logit_ref = outs
    tail = (CONV_WIDTH - 1) * nb

    xn = _rms(x, gmix_ref[...])
    proj = _dot(xn.astype(BF16), win_ref[...])
    u = proj[:, :S5_WIDTH]
    xb = proj[:, S5_WIDTH:S5_WIDTH + LRU_WIDTH]
    gate = proj[:, S5_WIDTH + LRU_WIDTH:]

    ub = u.astype(BF16)
    for j in range(2 * S5_CH // MXU_DIM):
        k0 = ((j * MXU_DIM) % S5_CH) // S5_STATE * S5_GROUP // MXU_DIM * MXU_DIM
        hbuf[:, j * MXU_DIM:(j + 1) * MXU_DIM] = _dot(
            ub[:, k0:k0 + MXU_DIM], wbu_ref[k0:k0 + MXU_DIM, j * MXU_DIM:(j + 1) * MXU_DIM])

    for cb in range(S5_CH // scan_cols):
        re_sl = slice(cb * scan_cols, (cb + 1) * scan_cols)
        im_sl = slice(S5_CH + cb * scan_cols, S5_CH + (cb + 1) * scan_cols)
        a_re = jnp.broadcast_to(are_ref[:, re_sl], (nb, scan_cols))
        a_im = jnp.broadcast_to(aim_ref[:, re_sl], (nb, scan_cols))

        def s5_step(t, carry, re_sl=re_sl, im_sl=im_sl, a_re=a_re, a_im=a_im):
            h_re, h_im = carry
            r0 = pl.multiple_of(t * nb, nb)
            n_re = a_re * h_re - a_im * h_im + hbuf[pl.ds(r0, nb), re_sl]
            n_im = a_re * h_im + a_im * h_re + hbuf[pl.ds(r0, nb), im_sl]
            hbuf[pl.ds(r0, nb), re_sl] = n_re
            hbuf[pl.ds(r0, nb), im_sl] = n_im
            return n_re, n_im

        h_re, h_im = lax.fori_loop(0, nq, s5_step, (s5st[:, re_sl], s5st[:, im_sl]))
        s5st[:, re_sl] = h_re
        s5st[:, im_sl] = h_im

    half = S5_CH // 2
    ys = []
    for j in range(2):
        h_re = hbuf[:, j * half:(j + 1) * half].astype(BF16)
        h_im = hbuf[:, S5_CH + j * half:S5_CH + (j + 1) * half].astype(BF16)
        n_sl = slice(j * MXU_DIM, (j + 1) * MXU_DIM)
        ys.append(_dot(h_re, wc_ref[j * half:(j + 1) * half, n_sl])
                  + _dot(h_im, wc_ref[S5_CH + j * half:S5_CH + (j + 1) * half, n_sl]))
    y = jnp.concatenate(ys, axis=-1) + d_ref[...] * u
    g = jax.nn.gelu(y)
    out_a = g * jax.nn.sigmoid(_dot(g.astype(BF16), wglu_ref[...]) + bglu_ref[...])

    xext[tail:tail + rows, :] = xb
    xc = convb_ref[...]
    for k in range(CONV_WIDTH):
        xc = xc + convw_ref[k:k + 1, :] * xext[k * nb:k * nb + rows, :]
    valid_rows = nq * nb
    if not isinstance(valid_rows, int):
        valid_rows = pl.multiple_of(valid_rows, nb)
    xext[0:tail, :] = xext[pl.ds(valid_rows, tail), :]
    ax = _dot(xc.astype(BF16), wax_ref[...])
    r = jax.nn.sigmoid(ax[:, :LRU_WIDTH] + ba_ref[...])
    i = jax.nn.sigmoid(ax[:, LRU_WIDTH:] + bx_ref[...])
    log_a = r * lsl_ref[...]
    a = jnp.exp(log_a)
    th = jnp.tanh(log_a)
    mult = jnp.sqrt(-2.0 * th / (1.0 - th))
    if first is not None:
        row = lax.broadcasted_iota(jnp.int32, (rows, 1), 0)
        mult = jnp.where(jnp.logical_and(first, row < nb), 1.0, mult)
    abuf[...] = a
    bbuf[...] = mult * (i * xc)

    def lru_step(t, h):
        r0 = pl.multiple_of(t * nb, nb)
        h = abuf[pl.ds(r0, nb), :] * h + bbuf[pl.ds(r0, nb), :]
        bbuf[pl.ds(r0, nb), :] = h
        return h

    lrust[...] = lax.fori_loop(0, nq, lru_step, lrust[...])
    out_b = bbuf[...] * jax.nn.gelu(gate)

    mixed = (_dot(_rms(out_a, gs5_ref[...]).astype(BF16), wout_ref[0:S5_WIDTH, :])
             + _dot(_rms(out_b, glru_ref[...]).astype(BF16), wout_ref[S5_WIDTH:, :]))
    hmid = x + mixed
    hmid_ref[...] = hmid
    xf = _rms(hmid, gffn_ref[...])
    xn_ref[...] = xf
    logits = jnp.dot(xf, wr_ref[...], preferred_element_type=F32,
                     precision=lax.Precision.HIGHEST) + br_ref[...]
    logit_ref[...] = logits.T[0:N_EXPERTS, :]


N_MIXER_WEIGHTS = 21


def _mixer_kernel(*refs, n_chunks, nq, nbp, nbs):
    xp_ref, meta_ref, xs_ref, s5h0_ref, lruh0_ref, conv0_ref = refs[:6]
    w = refs[6:6 + N_MIXER_WEIGHTS]
    (hmid_ref, xn_ref, logit_ref, ps5_ref, plru_ref, pconv_ref,
     ss5_ref, slru_ref, sconv_ref) = refs[6 + N_MIXER_WEIGHTS:15 + N_MIXER_WEIGHTS]
    (xt, hbuf, abuf, bbuf, s5st_p, lrust_p, xext_p, s5st_s, lrust_s, xext_s) = refs[15 + N_MIXER_WEIGHTS:]
    c = pl.program_id(0)
    rows = nbp * nq
    meta_rows = nbp * N_META
    tail_p = (CONV_WIDTH - 1) * nbp

    @pl.when(c == 0)
    def _():
        s5st_p[...] = jnp.zeros_like(s5st_p)
        lrust_p[...] = jnp.zeros_like(lrust_p)
        xext_p[0:tail_p, :] = jnp.zeros((tail_p, LRU_WIDTH), F32)
        for cb in range(D_MODEL // LANES):
            xt[cb, 0:meta_rows, :] = meta_ref[:, cb * LANES:(cb + 1) * LANES]
            xt[cb, meta_rows:rows, :] = jnp.zeros((rows - meta_rows, LANES), F32)

    @pl.when(jnp.logical_and(c >= 1, c <= n_chunks))
    def _():
        for b in range(nbp):
            for cb in range(D_MODEL // LANES):
                xt[cb, pl.ds(b, nq, stride=nbp), :] = xp_ref[b, :, cb * LANES:(cb + 1) * LANES]

    @pl.when(c <= n_chunks)
    def _():
        x_tm = jnp.concatenate([xt[cb] for cb in range(D_MODEL // LANES)], axis=-1)
        _mixer_rows(x_tm, c == 0, w, (s5st_p, lrust_p, xext_p), (hbuf, abuf, bbuf),
                    (hmid_ref, xn_ref, logit_ref), nb=nbp, nq=jnp.where(c == 0, N_META, nq),
                    rows=rows, scan_cols=1024)

    @pl.when(c == n_chunks)
    def _():
        ps5_ref[...] = s5st_p[...]
        plru_ref[...] = lrust_p[...]
        pconv_ref[...] = xext_p[0:tail_p, :]

    @pl.when(c == n_chunks + 1)
    def _():
        tail_s = (CONV_WIDTH - 1) * nbs
        s5st_s[...] = s5h0_ref[...]
        lrust_s[...] = lruh0_ref[...]
        xext_s[0:tail_s, :] = conv0_ref[...]
        _mixer_rows(xs_ref[...], None, w, (s5st_s, lrust_s, xext_s),
                    (hbuf.at[0:nbs], abuf.at[0:nbs], bbuf.at[0:nbs]),
                    (hmid_ref.at[0:nbs], xn_ref.at[0:nbs], logit_ref.at[:, 0:nbs]),
                    nb=nbs, nq=1, rows=nbs, scan_cols=128)
        logit_ref[:, nbs:rows] = jnp.zeros((N_EXPERTS, rows - nbs), F32)
        ss5_ref[...] = s5st_s[...]
        slru_ref[...] = lrust_s[...]
        sconv_ref[...] = xext_s[0:tail_s, :]


def _mixer_call(x_prompt, meta_tm, x_sample, s5h0, lruh0, conv0, weights):
    nbp, seq, _ = x_prompt.shape
    nbs = x_sample.shape[0]
    nq = PROMPT_STEPS
    n_chunks = seq // nq
    rows = nbp * nq
    n_tok = nbp * seq + nbs
    tail_p = (CONV_WIDTH - 1) * nbp
    tail_s = (CONV_WIDTH - 1) * nbs
    assert seq % nq == 0 and rows % LANES == 0 and nbs <= rows and nbs % LANES == 0
    assert meta_tm.shape == (nbp * N_META, D_MODEL) and nbp == SUBLANES

    def full(a):
        return pl.BlockSpec(a.shape, lambda c: (0,) * len(a.shape))

    def out_block(c):
        return jnp.clip(c - 1, 0, n_chunks)

    in_specs = ([pl.BlockSpec((nbp, nq, D_MODEL), lambda c: (0, jnp.clip(c - 1, 0, n_chunks - 1), 0)),
                 full(meta_tm), full(x_sample), full(s5h0), full(lruh0), full(conv0)]
                + [full(wt) for wt in weights])
    state_shapes = [(nbp, 2 * S5_CH), (nbp, LRU_WIDTH), (tail_p, LRU_WIDTH),
                    (nbs, 2 * S5_CH), (nbs, LRU_WIDTH), (tail_s, LRU_WIDTH)]
    out_shape = ([jax.ShapeDtypeStruct((n_tok, D_MODEL), F32), jax.ShapeDtypeStruct((n_tok, D_MODEL), F32),
                  jax.ShapeDtypeStruct((N_EXPERTS, (n_chunks + 1) * rows), F32)]
                 + [jax.ShapeDtypeStruct(s, F32) for s in state_shapes])
    out_specs = ([pl.BlockSpec((rows, D_MODEL), lambda c: (out_block(c), 0)),
                  pl.BlockSpec((rows, D_MODEL), lambda c: (out_block(c), 0)),
                  pl.BlockSpec((N_EXPERTS, rows), lambda c: (0, out_block(c)))]
                 + [full(s) for s in out_shape[3:]])
    return pl.pallas_call(
        functools.partial(_mixer_kernel, n_chunks=n_chunks, nq=nq, nbp=nbp, nbs=nbs),
        grid=(n_chunks + 2,),
        in_specs=in_specs,
        out_specs=out_specs,
        out_shape=out_shape,
        scratch_shapes=[
            pltpu.VMEM((D_MODEL // LANES, rows, LANES), F32),
            pltpu.VMEM((rows, 2 * S5_CH), F32),
            pltpu.VMEM((rows, LRU_WIDTH), F32),
            pltpu.VMEM((rows, LRU_WIDTH), F32),
            pltpu.VMEM((nbp, 2 * S5_CH), F32),
            pltpu.VMEM((nbp, LRU_WIDTH), F32),
            pltpu.VMEM((rows + tail_p, LRU_WIDTH), F32),
            pltpu.VMEM((nbs, 2 * S5_CH), F32),
            pltpu.VMEM((nbs, LRU_WIDTH), F32),
            pltpu.VMEM((nbs + tail_s, LRU_WIDTH), F32),
        ],
        compiler_params=pltpu.CompilerParams(dimension_semantics=("arbitrary",),
                                             vmem_limit_bytes=VMEM_LIMIT_BYTES),
        name="mixer",
    )(x_prompt, meta_tm, x_sample, s5h0, lruh0, conv0, *weights)


def _route_kernel(logit_ref, gates_ref, dest_ref, tile_e_ref, nused_ref, rowtok_hbm,
                  ids_s, rank_s, zeros_v, dest_sm, rowtok_ref, sems, out_sem,
                  *, n_tok, n_route, n_tiles, n_tiles_pad):
    ch = ROUTE_CHUNK
    n_chunks = n_route // ch
    n_rows = n_tiles * MOE_TILE
    iota_e = lax.broadcasted_iota(jnp.int32, (N_EXPERTS, ch), 0).astype(F32)
    lane_i = lax.broadcasted_iota(jnp.int32, (1, ch), 1)
    tri = (lax.broadcasted_iota(jnp.int32, (ch, ch), 0)
           <= lax.broadcasted_iota(jnp.int32, (ch, ch), 1)).astype(BF16)
    tile_shift = MOE_TILE.bit_length() - 1
    assert MOE_TILE == 1 << tile_shift

    zeros_v[...] = jnp.zeros_like(zeros_v)
    zero_copy = pltpu.make_async_copy(zeros_v, rowtok_ref, out_sem)
    zero_copy.start()
    zero_copy.wait()

    def chunk(i, carry):
        off = pl.multiple_of(i * ch, ch)
        l = logit_ref[:, pl.ds(off, ch)]
        real = (lane_i + i * ch) < n_tok
        vals, idxs, hots = [], [], []
        for _ in range(TOP_K):
            m = jnp.max(l, axis=0, keepdims=True)
            idx = jnp.min(jnp.where(l == m, iota_e, float(N_EXPERTS)), axis=0, keepdims=True)
            hot = iota_e == idx
            l = jnp.where(hot, -jnp.inf, l)
            vals.append(m)
            idxs.append(idx)
            hots.append(jnp.logical_and(hot, real))
        ex = [jnp.exp(v - vals[0]) for v in vals]
        den = ex[0] + ex[1] + ex[2] + ex[3]
        gates_ref[:, pl.ds(off, ch)] = jnp.concatenate([e / den for e in ex], axis=0)
        member = sum(h.astype(F32) for h in hots)
        c_incl = _dot(member.astype(BF16), tri) + carry
        c_excl = c_incl - member
        ranks = [jnp.sum(jnp.where(h, c_excl, 0.0), axis=0, keepdims=True) for h in hots]
        ids_s[:, pl.ds(off, ch)] = jnp.concatenate(idxs, axis=0)
        rank_s[:, pl.ds(off, ch)] = jnp.concatenate(ranks, axis=0)
        return c_incl[:, ch - 1:ch]

    counts = lax.fori_loop(0, n_chunks, chunk, jnp.zeros((N_EXPERTS, 1), F32))
    n_tile_e = jnp.floor((counts + (MOE_TILE - 1)) * (1.0 / MOE_TILE))
    ltri = (lax.broadcasted_iota(jnp.int32, (N_EXPERTS, N_EXPERTS), 0)
            >= lax.broadcasted_iota(jnp.int32, (N_EXPERTS, N_EXPERTS), 1)).astype(BF16)
    tile_end = _dot(ltri, jnp.broadcast_to(n_tile_e, (N_EXPERTS, LANES)).astype(BF16))
    tile_end = tile_end[:, 0:1]
    pstart = (tile_end - n_tile_e) * MOE_TILE
    tile_i = lax.broadcasted_iota(jnp.int32, (N_EXPERTS, n_tiles_pad), 1).astype(F32)
    te = jnp.sum((tile_end <= tile_i).astype(F32), axis=0, keepdims=True)
    tile_e_ref[...] = jnp.minimum(te, N_EXPERTS - 1).astype(jnp.int32)
    nused_ref[...] = jnp.broadcast_to(tile_end[N_EXPERTS - 1:N_EXPERTS, :], (1, LANES)).astype(jnp.int32)

    def dest_copy(i):
        off = pl.multiple_of(i * ch, ch)
        return pltpu.make_async_copy(dest_ref.at[:, pl.ds(off, ch)], dest_sm.at[i % 2], sems.at[i % 2])

    def invert(i):
        dest_copy(i).wait()
        slot = i % 2

        def body(j, _):
            for k in range(TOP_K):
                d = dest_sm[slot, k, j]
                rowtok_ref[d >> tile_shift, d & (MOE_TILE - 1)] = i * ch + j
            return 0

        lax.fori_loop(0, ch, body, 0, unroll=8)

    def chunk2(i, _):
        off = pl.multiple_of(i * ch, ch)
        ids = ids_s[:, pl.ds(off, ch)]
        rk = rank_s[:, pl.ds(off, ch)]
        real = (lane_i + i * ch) < n_tok
        rows = []
        for k in range(TOP_K):
            base = jnp.sum(jnp.where(iota_e == ids[k:k + 1, :], pstart, 0.0), axis=0, keepdims=True)
            rows.append(jnp.where(real, base + rk[k:k + 1, :], float(n_rows)))
        dest_ref[:, pl.ds(off, ch)] = jnp.concatenate(rows, axis=0).astype(jnp.int32)
        dest_copy(i).start()

        @pl.when(i > 0)
        def _():
            invert(i - 1)

        return 0

    lax.fori_loop(0, n_chunks, chunk2, 0)
    invert(n_chunks - 1)
    out_copy = pltpu.make_async_copy(rowtok_ref, rowtok_hbm, out_sem)
    out_copy.start()
    out_copy.wait()


def _route_call(logits_t, n_tok, n_tiles, n_tiles_pad):
    _, n_route = logits_t.shape
    assert n_route % ROUTE_CHUNK == 0
    return pl.pallas_call(
        functools.partial(_route_kernel, n_tok=n_tok, n_route=n_route, n_tiles=n_tiles, n_tiles_pad=n_tiles_pad),
        out_shape=[
            jax.ShapeDtypeStruct((TOP_K, n_route), F32),
            jax.ShapeDtypeStruct((TOP_K, n_route), jnp.int32),
            jax.ShapeDtypeStruct((1, n_tiles_pad), jnp.int32),
            jax.ShapeDtypeStruct((1, LANES), jnp.int32),
            jax.ShapeDtypeStruct((n_tiles + 1, MOE_TILE), jnp.int32),
        ],
        out_specs=[
            pl.BlockSpec(memory_space=pltpu.VMEM), pl.BlockSpec(memory_space=pltpu.VMEM),
            pl.BlockSpec(memory_space=pltpu.VMEM), pl.BlockSpec(memory_space=pltpu.VMEM),
            pl.BlockSpec(memory_space=pl.ANY),
        ],
        scratch_shapes=[pltpu.VMEM((TOP_K, n_route), F32), pltpu.VMEM((TOP_K, n_route), F32),
                        pltpu.VMEM((n_tiles + 1, MOE_TILE), jnp.int32),
                        pltpu.SMEM((2, TOP_K, ROUTE_CHUNK), jnp.int32),
                        pltpu.SMEM((n_tiles + 1, MOE_TILE), jnp.int32),
                        pltpu.SemaphoreType.DMA((2,)), pltpu.SemaphoreType.DMA],
        compiler_params=pltpu.CompilerParams(vmem_limit_bytes=VMEM_LIMIT_BYTES),
        name="route",
    )(logits_t)


def _expert_kernel(te_ref, nused_ref, tok_ref, tok_next_ref, xn_ref, wup_ref, bup_ref, wdn_ref, bdn_ref,
                   ys_ref, xs_buf, sems):
    i = pl.program_id(0)
    n_used = nused_ref[0]
    slot = i % 2
    tm = MOE_TILE

    def row_copy(t_ref, s, j):
        return pltpu.make_async_copy(xn_ref.at[pl.ds(t_ref[0, 0, j], 1), :], xs_buf.at[s, pl.ds(j, 1), :],
                                     sems.at[s])

    def wait_tile(s):
        pltpu.make_async_copy(xn_ref.at[pl.ds(0, tm), :], xs_buf.at[s], sems.at[s]).wait()

    @pl.when(i == 0)
    def _():
        def body(j, _):
            row_copy(tok_ref, 0, j).start()
            return 0
        lax.fori_loop(0, tm, body, 0, unroll=8)

    @pl.when(i < n_used)
    def _():
        wait_tile(slot)
        def issue_next(j, _):
            row_copy(tok_next_ref, 1 - slot, j).start()
            return 0
        lax.fori_loop(0, tm, issue_next, 0, unroll=8)
        xb = xs_buf[slot].astype(BF16)
        hu = _dot(xb, wup_ref[0].astype(BF16)) + bup_ref[0]
        glu = jnp.minimum(hu[:, :D_FF], SWIGLU_LIMIT)
        lin = jnp.clip(hu[:, D_FF:], -SWIGLU_LIMIT, SWIGLU_LIMIT)
        act = glu * jax.nn.sigmoid(SWIGLU_ALPHA * glu) * (lin + 1.0)
        ys_ref[...] = _dot(act.astype(BF16), wdn_ref[0].astype(BF16)) + bdn_ref[0]

    @pl.when(i == n_used - 1)
    def _():
        wait_tile(1 - slot)

    @pl.when(i >= n_used)
    def _():
        ys_ref[...] = jnp.zeros_like(ys_ref)


def _expert_call(tile_e, n_used, row_tok, xn, w_up, b_up, w_down, b_down, n_tiles):
    tm = MOE_TILE

    def w_map(i, te, nu):
        return (te[i], 0, 0)

    grid_spec = pltpu.PrefetchScalarGridSpec(
        num_scalar_prefetch=2,
        grid=(n_tiles,),
        in_specs=[
            pl.BlockSpec((1, 1, tm), lambda i, te, nu: (i, 0, 0), memory_space=pltpu.SMEM),
            pl.BlockSpec((1, 1, tm), lambda i, te, nu: (jnp.minimum(i + 1, nu[0] - 1), 0, 0),
                         memory_space=pltpu.SMEM),
            pl.BlockSpec(memory_space=pl.ANY),
            pl.BlockSpec((1, D_MODEL, 2 * D_FF), w_map),
            pl.BlockSpec((1, 1, 2 * D_FF), w_map),
            pl.BlockSpec((1, D_FF, D_MODEL), w_map),
            pl.BlockSpec((1, 1, D_MODEL), w_map),
        ],
        out_specs=pl.BlockSpec((tm, D_MODEL), lambda i, te, nu: (i, 0)),
        scratch_shapes=[pltpu.VMEM((2, tm, D_MODEL), F32), pltpu.SemaphoreType.DMA((2,))],
    )
    return pl.pallas_call(
        _expert_kernel,
        grid_spec=grid_spec,
        out_shape=jax.ShapeDtypeStruct((n_tiles * tm, D_MODEL), F32),
        compiler_params=pltpu.CompilerParams(dimension_semantics=("arbitrary",),
                                             vmem_limit_bytes=VMEM_LIMIT_BYTES),
        name="experts",
    )(tile_e, n_used, row_tok.reshape(-1, 1, tm), row_tok.reshape(-1, 1, tm),
      xn, w_up, b_up, w_down, b_down)


def _combine_kernel(dest_ref, dest_next_ref, hmid_ref, gates_ref, gf_ref, ys_ref, yp_ref, ysm_ref,
                    ybuf, res, sems, *, nbp):
    i = pl.program_id(0)
    n = pl.num_programs(0)
    slot = i % 2
    tt = TOKEN_TILE

    def issue(d_ref, s):
        def body(j, _):
            for k in range(TOP_K):
                d = d_ref[k, j]
                pltpu.make_async_copy(ys_ref.at[pl.ds(d, 1), :], ybuf.at[s, k, pl.ds(j, 1), :],
                                      sems.at[s]).start()
            return 0
        lax.fori_loop(0, tt, body, 0, unroll=8)

    @pl.when(i == 0)
    def _():
        issue(dest_ref, 0)

    @pl.when(i + 1 < n)
    def _():
        issue(dest_next_ref, 1 - slot)

    for k in range(TOP_K):
        pltpu.make_async_copy(ys_ref.at[pl.ds(0, tt), :], ybuf.at[slot, k], sems.at[slot]).wait()

    g = gates_ref[...]
    g_t = jnp.concatenate([g, jnp.zeros((tt - TOP_K, tt), F32)], axis=0).T
    acc = hmid_ref[...]
    for k in range(TOP_K):
        acc = acc + g_t[:, k:k + 1] * ybuf[slot, k]
    out = _rms(acc, gf_ref[...])
    n_cb = D_MODEL // LANES

    @pl.when(i < n - 1)
    def _():
        for cb in range(n_cb):
            res[cb] = out[:, cb * LANES:(cb + 1) * LANES]
        for b in range(nbp):
            yp_ref[b] = jnp.concatenate(
                [res[cb, pl.ds(b, tt // nbp, stride=nbp), :] for cb in range(n_cb)], axis=-1)

    @pl.when(i == n - 1)
    def _():
        ysm_ref[...] = out


def _combine_call(dest, hmid, gates, gf, ys, nbp, seq, nbs):
    n_tok, _ = hmid.shape
    tt = TOKEN_TILE
    n = n_tok // tt
    assert nbs == tt and n_tok == nbp * seq + nbs and (nbp * seq) % tt == 0
    steps = tt // nbp
    return pl.pallas_call(
        functools.partial(_combine_kernel, nbp=nbp),
        grid=(n,),
        in_specs=[
            pl.BlockSpec((TOP_K, tt), lambda i: (0, i), memory_space=pltpu.SMEM),
            pl.BlockSpec((TOP_K, tt), lambda i: (0, jnp.minimum(i + 1, n - 1)), memory_space=pltpu.SMEM),
            pl.BlockSpec((tt, D_MODEL), lambda i: (i, 0)),
            pl.BlockSpec((TOP_K, tt), lambda i: (0, i)),
            pl.BlockSpec((1, D_MODEL), lambda i: (0, 0)),
            pl.BlockSpec(memory_space=pl.ANY),
        ],
        out_specs=[
            pl.BlockSpec((nbp, steps, D_MODEL), lambda i: (0, jnp.minimum(i, n - 2), 0)),
            pl.BlockSpec((nbs, D_MODEL), lambda i: (0, 0)),
        ],
        out_shape=[jax.ShapeDtypeStruct((nbp, seq, D_MODEL), F32), jax.ShapeDtypeStruct((nbs, D_MODEL), F32)],
        scratch_shapes=[pltpu.VMEM((2, TOP_K, tt, D_MODEL), F32), pltpu.VMEM((D_MODEL // LANES, tt, LANES), F32),
                        pltpu.SemaphoreType.DMA((2,))],
        compiler_params=pltpu.CompilerParams(dimension_semantics=("arbitrary",),
                                             vmem_limit_bytes=VMEM_LIMIT_BYTES),
        name="combine",
    )(dest, dest, hmid, gates, gf, ys)


def _block_diag(blocks):
    n, r, c = blocks.shape
    eye = jnp.eye(n, dtype=blocks.dtype)
    return (blocks[:, :, None, :] * eye[:, None, :, None]).reshape(n * r, n * c)


def _mixer_weights(norm_mix_g, w_in, lam_re, lam_im, log_step, b_re, b_im, c_re, c_im, d, w_glu, b_glu,
                   conv_w, conv_b, w_a, b_a, w_x, b_x, lam, g_s5, g_lru, w_out, g_ffn, w_router, b_router):
    step = jnp.exp(log_step)[:, None]
    dr = lam_re * step
    di = lam_im * step
    mag = jnp.exp(dr)
    lb_re = mag * jnp.cos(di)
    lb_im = mag * jnp.sin(di)
    den = lam_re * lam_re + lam_im * lam_im
    num_re = lb_re - 1.0
    coef_re = (num_re * lam_re + lb_im * lam_im) / den
    coef_im = (lb_im * lam_re - num_re * lam_im) / den
    bb_re = coef_re[..., None] * b_re - coef_im[..., None] * b_im
    bb_im = coef_re[..., None] * b_im + coef_im[..., None] * b_re
    wbu = jnp.concatenate([_block_diag(bb_re.transpose(0, 2, 1)),
                           _block_diag(bb_im.transpose(0, 2, 1))], axis=1).astype(BF16)
    wc = jnp.concatenate([_block_diag(c_re.transpose(0, 2, 1)),
                          -_block_diag(c_im.transpose(0, 2, 1))], axis=0).astype(BF16)
    wax = jnp.concatenate([_block_diag(w_a), _block_diag(w_x)], axis=1).astype(BF16)
    lsl = LRU_C * jax.nn.log_sigmoid(lam)
    wr = jnp.pad(w_router, ((0, 0), (0, LANES - N_EXPERTS)))
    br = jnp.pad(b_router, (0, LANES - N_EXPERTS))
    row = lambda v: v.reshape(1, -1)
    weights = [row(norm_mix_g), w_in.astype(BF16), wbu, row(lb_re), row(lb_im), wc, row(d),
               w_glu.astype(BF16), row(b_glu), conv_w, row(conv_b), wax, row(b_a), row(b_x), row(lsl),
               row(g_s5), row(g_lru), w_out.astype(BF16), row(g_ffn), wr, row(br)]
    assert len(weights) == N_MIXER_WEIGHTS
    return weights


def kernel(x_prompt, x_sample, state_s5_re, state_s5_im, state_lru_h, state_conv, meta_tokens,
           norm_mix_g, w_in, s5_lambda_re, s5_lambda_im, s5_log_step, s5_b_re, s5_b_im, s5_c_re, s5_c_im,
           s5_d, s5_w_glu, s5_b_glu, conv_w, conv_b, lru_w_a, lru_b_a, lru_w_x, lru_b_x, lru_lambda,
           norm_s5_g, norm_lru_g, w_out, norm_ffn_g, w_router, b_router, w_up, b_up, w_down, b_down, norm_f_g):
    bp, seq, _ = x_prompt.shape
    bs = x_sample.shape[0]
    out_dtype = x_prompt.dtype
    n_tok = bp * seq + bs
    assert x_sample.shape[1] == 1 and norm_mix_g.shape[0] == 1

    weights = _mixer_weights(*[a[0].astype(F32) for a in (
        norm_mix_g, w_in, s5_lambda_re, s5_lambda_im, s5_log_step, s5_b_re, s5_b_im, s5_c_re, s5_c_im,
        s5_d, s5_w_glu, s5_b_glu, conv_w, conv_b, lru_w_a, lru_b_a, lru_w_x, lru_b_x, lru_lambda,
        norm_s5_g, norm_lru_g, w_out, norm_ffn_g, w_router, b_router)])

    meta_tm = jnp.broadcast_to(meta_tokens.astype(F32)[:, None, :], (N_META, bp, D_MODEL)).reshape(-1, D_MODEL)
    s5h0 = jnp.concatenate([state_s5_re[0].reshape(bs, S5_CH), state_s5_im[0].reshape(bs, S5_CH)],
                           axis=1).astype(F32)
    conv0 = state_conv[0].astype(F32).transpose(1, 0, 2).reshape((CONV_WIDTH - 1) * bs, LRU_WIDTH)
    (hmid, xn, logits_t, p_s5, p_lru, p_conv, s_s5, s_lru, s_conv) = _mixer_call(
        x_prompt.astype(F32), meta_tm, x_sample.astype(F32).reshape(bs, D_MODEL),
        s5h0, state_lru_h[0].astype(F32), conv0, weights)

    assert n_tok % TOKEN_TILE == 0 and (n_tok * TOP_K) % MOE_TILE == 0
    n_tiles = n_tok * TOP_K // MOE_TILE + N_EXPERTS
    n_tiles_pad = -(-n_tiles // LANES) * LANES
    gates, dest, tile_e, n_used, row_tok = _route_call(logits_t, n_tok, n_tiles, n_tiles_pad)
    ys = _expert_call(tile_e.reshape(-1), n_used.reshape(-1)[:1], row_tok, xn, w_up[0], b_up[0][:, None, :],
                      w_down[0], b_down[0][:, None, :], n_tiles)
    y_prompt, y_sample = _combine_call(dest, hmid, gates, norm_f_g.astype(F32).reshape(1, D_MODEL), ys,
                                       bp, seq, bs)

    def s5_parts(st, b):
        return (st[:, :S5_CH].reshape(1, b, S5_GROUPS, S5_STATE), st[:, S5_CH:].reshape(1, b, S5_GROUPS, S5_STATE))

    def conv_part(cv, b):
        return cv.reshape(CONV_WIDTH - 1, b, LRU_WIDTH).transpose(1, 0, 2)[None]

    p_re, p_im = s5_parts(p_s5, bp)
    s_re, s_im = s5_parts(s_s5, bs)
    return (y_prompt.astype(out_dtype), y_sample.reshape(bs, 1, D_MODEL).astype(out_dtype),
            p_re, p_im, p_lru[None], conv_part(p_conv, bp),
            s_re, s_im, s_lru[None], conv_part(s_conv, bs))
```

```python
import functools

import jax
import jax.numpy as jnp
from jax import lax
from jax.experimental import pallas as pl
from jax.experimental.pallas import tpu as pltpu

D_MODEL = 1024
N_META = 16
S5_WIDTH = 512
LRU_WIDTH = 512
S5_GROUP = 16
S5_GROUPS = 32
S5_STATE = 64
S5_CH = S5_GROUPS * S5_STATE
CONV_WIDTH = 4
LRU_C = 8.0
N_EXPERTS = 32
TOP_K = 4
D_FF = 1024
SWIGLU_ALPHA = 1.702
SWIGLU_LIMIT = 7.0
EPS = 1e-6

SUBLANES = 8
LANES = 128
MXU_DIM = 256
VMEM_LIMIT_BYTES = 56 * 1024 * 1024

PROMPT_STEPS = 64
MOE_TILE = 256
ROUTE_CHUNK = 256
TOKEN_TILE = 128

F32 = jnp.float32
BF16 = jnp.bfloat16

assert D_MODEL == SUBLANES * LANES


def _rms(x, g):
    return x * lax.rsqrt(jnp.mean(x * x, axis=-1, keepdims=True) + EPS) * g


def _dot(a, b):
    return jnp.dot(a, b, preferred_element_type=F32)


def _mixer_rows(x, first, w, state, scratch, outs, *, nb, nq, rows, scan_cols):
    (gmix_ref, win_ref, wbu_ref, are_ref, aim_ref, wc_ref, d_ref, wglu_ref, bglu_ref,
     convw_ref, convb_ref, wax_ref, ba_ref, bx_ref, lsl_ref, gs5_ref, glru_ref, wout_ref,
     gffn_ref, wr_ref, br_ref) = w
    s5st, lrust, xext = state
    hbuf, abuf, bbuf = scratch
    hmid_ref, xn_ref, logit_ref = outs
    tail = (CONV_WIDTH - 1) * nb

    xn = _rms(x, gmix_ref[...])
    proj = _dot(xn.astype(BF16), win_ref[...])
    u = proj[:, :S5_WIDTH]
    xb = proj[:, S5_WIDTH:S5_WIDTH + LRU_WIDTH]
    gate = proj[:, S5_WIDTH + LRU_WIDTH:]

    ub = u.astype(BF16)
    for j in range(2 * S5_CH // MXU_DIM):
        k0 = ((j * MXU_DIM) % S5_CH) // S5_STATE * S5_GROUP // MXU_DIM * MXU_DIM
        hbuf[:, j * MXU_DIM:(j + 1) * MXU_DIM] = _dot(
            ub[:, k0:k0 + MXU_DIM], wbu_ref[k0:k0 + MXU_DIM, j * MXU_DIM:(j + 1) * MXU_DIM])

    for cb in range(S5_CH // scan_cols):
        re_sl = slice(cb * scan_cols, (cb + 1) * scan_cols)
        im_sl = slice(S5_CH + cb * scan_cols, S5_CH + (cb + 1) * scan_cols)
        a_re = jnp.broadcast_to(are_ref[:, re_sl], (nb, scan_cols))
        a_im = jnp.broadcast_to(aim_ref[:, re_sl], (nb, scan_cols))

        def s5_step(t, carry, re_sl=re_sl, im_sl=im_sl, a_re=a_re, a_im=a_im):
            h_re, h_im = carry
            r0 = pl.multiple_of(t * nb, nb)
            n_re = a_re * h_re - a_im * h_im + hbuf[pl.ds(r0, nb), re_sl]
            n_im = a_re * h_im + a_im * h_re + hbuf[pl.ds(r0, nb), im_sl]
            hbuf[pl.ds(r0, nb), re_sl] = n_re
            hbuf[pl.ds(r0, nb), im_sl] = n_im
            return n_re, n_im

        h_re, h_im = lax.fori_loop(0, nq, s5_step, (s5st[:, re_sl], s5st[:, im_sl]))
        s5st[:, re_sl] = h_re
        s5st[:, im_sl] = h_im

    half = S5_CH // 2
    ys = []
    for j in range(2):
        h_re = hbuf[:, j * half:(j + 1) * half].astype(BF16)
        h_im = hbuf[:, S5_CH + j * half:S5_CH + (j + 1) * half].astype(BF16)
        n_sl = slice(j * MXU_DIM, (j + 1) * MXU_DIM)
        ys.append(_dot(h_re, wc_ref[j * half:(j + 1) * half, n_sl])
                  + _dot(h_im, wc_ref[S5_CH + j * half:S5_CH + (j + 1) * half, n_sl]))
    y = jnp.concatenate(ys, axis=-1) + d_ref[...] * u
    g = jax.nn.gelu(y)
    out_a = g * jax.nn.sigmoid(_dot(g.astype(BF16), wglu_ref[...]) + bglu_ref[...])

    xext[tail:tail + rows, :] = xb
    xc = convb_ref[...]
    for k in range(CONV_WIDTH):
        xc = xc + convw_ref[k:k + 1, :] * xext[k * nb:k * nb + rows, :]
    valid_rows = nq * nb
    if not isinstance(valid_rows, int):
        valid_rows = pl.multiple_of(valid_rows, nb)
    xext[0:tail, :] = xext[pl.ds(valid_rows, tail), :]
    ax = _dot(xc.astype(BF16), wax_ref[...])
    r = jax.nn.sigmoid(ax[:, :LRU_WIDTH] + ba_ref[...])
    i = jax.nn.sigmoid(ax[:, LRU_WIDTH:] + bx_ref[...])
    log_a = r * lsl_ref[...]
    a = jnp.exp(log_a)
    th = jnp.tanh(log_a)
    mult = jnp.sqrt(-2.0 * th / (1.0 - th))
    if first is not None:
        row = lax.broadcasted_iota(jnp.int32, (rows, 1), 0)
        mult = jnp.where(jnp.logical_and(first, row < nb), 1.0, mult)
    abuf[...] = a
    bbuf[...] = mult * (i * xc)

    def lru_step(t, h):
        r0 = pl.multiple_of(t * nb, nb)
        h = abuf[pl.ds(r0, nb), :] * h + bbuf[pl.ds(r0, nb), :]
        bbuf[pl.ds(r0, nb), :] = h
        return h

    lrust[...] = lax.fori_loop(0, nq, lru_step, lrust[...])
    out_b = bbuf[...] * jax.nn.gelu(gate)

    mixed = (_dot(_rms(out_a, gs5_ref[...]).astype(BF16), wout_ref[0:S5_WIDTH, :])
             + _dot(_rms(out_b, glru_ref[...]).astype(BF16), wout_ref[S5_WIDTH:, :]))
    hmid = x + mixed
    hmid_ref[...] = hmid
    xf = _rms(hmid, gffn_ref[...])
    for s in range(D_MODEL // LANES):
        xn_ref[pl.ds(s, rows, stride=SUBLANES), :] = xf[:, s * LANES:(s + 1) * LANES]
    x_hi = xf.astype(BF16)
    x_lo = (xf - x_hi.astype(F32)).astype(BF16)
    w_hi = wr_ref[...].astype(BF16)
    w_split = jnp.concatenate([w_hi, (wr_ref[...] - w_hi.astype(F32)).astype(BF16)], axis=-1)
    p_hi = _dot(x_hi, w_split)
    p_lo = _dot(x_lo, w_split)
    logits = (p_hi[:, :LANES] + p_hi[:, LANES:]) + (p_lo[:, :LANES] + p_lo[:, LANES:]) + br_ref[...]
    logit_ref[...] = logits.T[0:N_EXPERTS, :]


N_MIXER_WEIGHTS = 21


def _mixer_kernel(*refs, n_chunks, nq, nbp, nbs):
    xp_ref, meta_ref, xs_ref, s5h0_ref, lruh0_ref, conv0_ref = refs[:6]
    w = refs[6:6 + N_MIXER_WEIGHTS]
    (hmid_ref, xn_ref, logit_ref, ps5_ref, plru_ref, pconv_ref,
     ss5_ref, slru_ref, sconv_ref) = refs[6 + N_MIXER_WEIGHTS:15 + N_MIXER_WEIGHTS]
    (xt, hbuf, abuf, bbuf, s5st_p, lrust_p, xext_p, s5st_s, lrust_s, xext_s) = refs[15 + N_MIXER_WEIGHTS:]
    c = pl.program_id(0)
    rows = nbp * nq
    meta_rows = nbp * N_META
    tail_p = (CONV_WIDTH - 1) * nbp

    @pl.when(c == 0)
    def _():
        s5st_p[...] = jnp.zeros_like(s5st_p)
        lrust_p[...] = jnp.zeros_like(lrust_p)
        xext_p[0:tail_p, :] = jnp.zeros((tail_p, LRU_WIDTH), F32)
        for cb in range(D_MODEL // LANES):
            xt[cb, 0:meta_rows, :] = meta_ref[:, cb * LANES:(cb + 1) * LANES]
            xt[cb, meta_rows:rows, :] = jnp.zeros((rows - meta_rows, LANES), F32)

    @pl.when(jnp.logical_and(c >= 1, c <= n_chunks))
    def _():
        for b in range(nbp):
            for cb in range(D_MODEL // LANES):
                xt[cb, pl.ds(b, nq, stride=nbp), :] = xp_ref[b, :, cb * LANES:(cb + 1) * LANES]

    @pl.when(c <= n_chunks)
    def _():
        x_tm = jnp.concatenate([xt[cb] for cb in range(D_MODEL // LANES)], axis=-1)
        _mixer_rows(x_tm, c == 0, w, (s5st_p, lrust_p, xext_p), (hbuf, abuf, bbuf),
                    (hmid_ref, xn_ref, logit_ref), nb=nbp, nq=jnp.where(c == 0, N_META, nq),
                    rows=rows, scan_cols=1024)

    @pl.when(c == n_chunks)
    def _():
        ps5_ref[...] = s5st_p[...]
        plru_ref[...] = lrust_p[...]
        pconv_ref[...] = xext_p[0:tail_p, :]

    @pl.when(c == n_chunks + 1)
    def _():
        tail_s = (CONV_WIDTH - 1) * nbs
        s5st_s[...] = s5h0_ref[...]
        lrust_s[...] = lruh0_ref[...]
        xext_s[0:tail_s, :] = conv0_ref[...]
        _mixer_rows(xs_ref[...], None, w, (s5st_s, lrust_s, xext_s),
                    (hbuf.at[0:nbs], abuf.at[0:nbs], bbuf.at[0:nbs]),
                    (hmid_ref.at[0:nbs], xn_ref.at[0:nbs * SUBLANES], logit_ref.at[:, 0:nbs]),
                    nb=nbs, nq=1, rows=nbs, scan_cols=128)
        logit_ref[:, nbs:rows] = jnp.zeros((N_EXPERTS, rows - nbs), F32)
        ss5_ref[...] = s5st_s[...]
        slru_ref[...] = lrust_s[...]
        sconv_ref[...] = xext_s[0:tail_s, :]


def _mixer_call(x_prompt, meta_tm, x_sample, s5h0, lruh0, conv0, weights):
    nbp, seq, _ = x_prompt.shape
    nbs = x_sample.shape[0]
    nq = PROMPT_STEPS
    n_chunks = seq // nq
    rows = nbp * nq
    n_tok = nbp * seq + nbs
    tail_p = (CONV_WIDTH - 1) * nbp
    tail_s = (CONV_WIDTH - 1) * nbs
    assert seq % nq == 0 and rows % LANES == 0 and nbs <= rows and nbs % LANES == 0
    assert meta_tm.shape == (nbp * N_META, D_MODEL) and nbp == SUBLANES

    def full(a):
        return pl.BlockSpec(a.shape, lambda c: (0,) * len(a.shape))

    def out_block(c):
        return jnp.clip(c - 1, 0, n_chunks)

    in_specs = ([pl.BlockSpec((nbp, nq, D_MODEL), lambda c: (0, jnp.clip(c - 1, 0, n_chunks - 1), 0)),
                 full(meta_tm), full(x_sample), full(s5h0), full(lruh0), full(conv0)]
                + [full(wt) for wt in weights])
    state_shapes = [(nbp, 2 * S5_CH), (nbp, LRU_WIDTH), (tail_p, LRU_WIDTH),
                    (nbs, 2 * S5_CH), (nbs, LRU_WIDTH), (tail_s, LRU_WIDTH)]
    out_shape = ([jax.ShapeDtypeStruct((n_tok, D_MODEL), F32),
                  jax.ShapeDtypeStruct((n_tok * SUBLANES, LANES), F32),
                  jax.ShapeDtypeStruct((N_EXPERTS, (n_chunks + 1) * rows), F32)]
                 + [jax.ShapeDtypeStruct(s, F32) for s in state_shapes])
    out_specs = ([pl.BlockSpec((rows, D_MODEL), lambda c: (out_block(c), 0)),
                  pl.BlockSpec((rows * SUBLANES, LANES), lambda c: (out_block(c), 0)),
                  pl.BlockSpec((N_EXPERTS, rows), lambda c: (0, out_block(c)))]
                 + [full(s) for s in out_shape[3:]])
    return pl.pallas_call(
        functools.partial(_mixer_kernel, n_chunks=n_chunks, nq=nq, nbp=nbp, nbs=nbs),
        grid=(n_chunks + 2,),
        in_specs=in_specs,
        out_specs=out_specs,
        out_shape=out_shape,
        scratch_shapes=[
            pltpu.VMEM((D_MODEL // LANES, rows, LANES), F32),
            pltpu.VMEM((rows, 2 * S5_CH), F32),
            pltpu.VMEM((rows, LRU_WIDTH), F32),
            pltpu.VMEM((rows, LRU_WIDTH), F32),
            pltpu.VMEM((nbp, 2 * S5_CH), F32),
            pltpu.VMEM((nbp, LRU_WIDTH), F32),
            pltpu.VMEM((rows + tail_p, LRU_WIDTH), F32),
            pltpu.VMEM((nbs, 2 * S5_CH), F32),
            pltpu.VMEM((nbs, LRU_WIDTH), F32),
            pltpu.VMEM((nbs + tail_s, LRU_WIDTH), F32),
        ],
        compiler_params=pltpu.CompilerParams(dimension_semantics=("arbitrary",),
                                             vmem_limit_bytes=VMEM_LIMIT_BYTES),
        name="mixer",
    )(x_prompt, meta_tm, x_sample, s5h0, lruh0, conv0, *weights)


def _route_kernel(logit_ref, gates_ref, dest_ref, tile_e_ref, nused_ref, rowtok_hbm,
                  ids_s, rank_s, zeros_v, dest_sm, rowtok_ref, sems, out_sem,
                  *, n_tok, n_route, n_tiles, n_tiles_pad):
    ch = ROUTE_CHUNK
    n_chunks = n_route // ch
    n_rows = n_tiles * MOE_TILE
    iota_e = lax.broadcasted_iota(jnp.int32, (N_EXPERTS, ch), 0).astype(F32)
    lane_i = lax.broadcasted_iota(jnp.int32, (1, ch), 1)
    tri = (lax.broadcasted_iota(jnp.int32, (ch, ch), 0)
           <= lax.broadcasted_iota(jnp.int32, (ch, ch), 1)).astype(BF16)
    zeros_v[...] = jnp.zeros_like(zeros_v)
    zero_copy = pltpu.make_async_copy(zeros_v, rowtok_ref, out_sem)
    zero_copy.start()
    zero_copy.wait()

    def chunk(i, carry):
        off = pl.multiple_of(i * ch, ch)
        l = logit_ref[:, pl.ds(off, ch)]
        real = (lane_i + i * ch) < n_tok
        vals, idxs, hots = [], [], []
        for _ in range(TOP_K):
            m = jnp.max(l, axis=0, keepdims=True)
            idx = jnp.min(jnp.where(l == m, iota_e, float(N_EXPERTS)), axis=0, keepdims=True)
            hot = iota_e == idx
            l = jnp.where(hot, -jnp.inf, l)
            vals.append(m)
            idxs.append(idx)
            hots.append(jnp.logical_and(hot, real))
        ex = [jnp.exp(v - vals[0]) for v in vals]
        den = ex[0] + ex[1] + ex[2] + ex[3]
        gates_ref[:, pl.ds(off, ch)] = jnp.concatenate([e / den for e in ex], axis=0)
        member = sum(h.astype(F32) for h in hots)
        c_incl = _dot(member.astype(BF16), tri) + carry
        c_excl = c_incl - member
        ranks = [jnp.sum(jnp.where(h, c_excl, 0.0), axis=0, keepdims=True) for h in hots]
        ids_s[:, pl.ds(off, ch)] = jnp.concatenate(idxs, axis=0)
        rank_s[:, pl.ds(off, ch)] = jnp.concatenate(ranks, axis=0)
        return c_incl[:, ch - 1:ch]

    counts = lax.fori_loop(0, n_chunks, chunk, jnp.zeros((N_EXPERTS, 1), F32))
    n_tile_e = jnp.floor((counts + (MOE_TILE - 1)) * (1.0 / MOE_TILE))
    ltri = (lax.broadcasted_iota(jnp.int32, (N_EXPERTS, N_EXPERTS), 0)
            >= lax.broadcasted_iota(jnp.int32, (N_EXPERTS, N_EXPERTS), 1)).astype(BF16)
    tile_end = _dot(ltri, jnp.broadcast_to(n_tile_e, (N_EXPERTS, LANES)).astype(BF16))
    tile_end = tile_end[:, 0:1]
    pstart = (tile_end - n_tile_e) * MOE_TILE
    tile_i = lax.broadcasted_iota(jnp.int32, (N_EXPERTS, n_tiles_pad), 1).astype(F32)
    te = jnp.sum((tile_end <= tile_i).astype(F32), axis=0, keepdims=True)
    tile_e_ref[...] = jnp.minimum(te, N_EXPERTS - 1).astype(jnp.int32)
    nused_ref[...] = jnp.broadcast_to(tile_end[N_EXPERTS - 1:N_EXPERTS, :], (1, LANES)).astype(jnp.int32)

    halves = ch // LANES

    def dest_copy(i, h):
        off = pl.multiple_of(i * ch + h * LANES, LANES)
        buf = (i % 2) * halves + h
        return pltpu.make_async_copy(dest_ref.at[:, pl.ds(off, LANES)], dest_sm.at[buf], sems.at[buf])

    def invert(i):
        group = 4
        for h in range(halves):
            dest_copy(i, h).wait()
            buf = (i % 2) * halves + h
            tok0 = i * ch + h * LANES

            for j0 in range(0, LANES, group):
                ds = [[dest_sm[buf, k, j0 + u] for k in range(TOP_K)] for u in range(group)]
                for u in range(group):
                    for k in range(TOP_K):
                        rowtok_ref[ds[u][k]] = tok0 + (j0 + u)

    def chunk2(i, _):
        off = pl.multiple_of(i * ch, ch)
        ids = ids_s[:, pl.ds(off, ch)]
        rk = rank_s[:, pl.ds(off, ch)]
        real = (lane_i + i * ch) < n_tok
        rows = []
        for k in range(TOP_K):
            base = jnp.sum(jnp.where(iota_e == ids[k:k + 1, :], pstart, 0.0), axis=0, keepdims=True)
            rows.append(jnp.where(real, base + rk[k:k + 1, :], float(n_rows)))
        dest_ref[:, pl.ds(off, ch)] = jnp.concatenate(rows, axis=0).astype(jnp.int32)
        for h in range(halves):
            dest_copy(i, h).start()

        @pl.when(i > 0)
        def _():
            invert(i - 1)

        return 0

    lax.fori_loop(0, n_chunks, chunk2, 0)
    invert(n_chunks - 1)
    out_copy = pltpu.make_async_copy(rowtok_ref, rowtok_hbm, out_sem)
    out_copy.start()
    out_copy.wait()


def _row_tok_len(n_tiles):
    flat_tile = SUBLANES * LANES
    return -(-(n_tiles + 1) * MOE_TILE // flat_tile) * flat_tile


def _route_call(logits_t, n_tok, n_tiles, n_tiles_pad):
    _, n_route = logits_t.shape
    assert n_route % ROUTE_CHUNK == 0 and ROUTE_CHUNK % LANES == 0
    n_alloc = _row_tok_len(n_tiles)
    n_bufs = 2 * (ROUTE_CHUNK // LANES)
    return pl.pallas_call(
        functools.partial(_route_kernel, n_tok=n_tok, n_route=n_route, n_tiles=n_tiles, n_tiles_pad=n_tiles_pad),
        out_shape=[
            jax.ShapeDtypeStruct((TOP_K, n_route), F32),
            jax.ShapeDtypeStruct((TOP_K, n_route), jnp.int32),
            jax.ShapeDtypeStruct((1, n_tiles_pad), jnp.int32),
            jax.ShapeDtypeStruct((1, LANES), jnp.int32),
            jax.ShapeDtypeStruct((n_alloc,), jnp.int32),
        ],
        out_specs=[
            pl.BlockSpec(memory_space=pltpu.VMEM), pl.BlockSpec(memory_space=pltpu.VMEM),
            pl.BlockSpec(memory_space=pltpu.VMEM), pl.BlockSpec(memory_space=pltpu.VMEM),
            pl.BlockSpec(memory_space=pl.ANY),
        ],
        scratch_shapes=[pltpu.VMEM((TOP_K, n_route), F32), pltpu.VMEM((TOP_K, n_route), F32),
                        pltpu.VMEM((n_alloc,), jnp.int32),
                        pltpu.SMEM((n_bufs, TOP_K, LANES), jnp.int32),
                        pltpu.SMEM((n_alloc,), jnp.int32),
                        pltpu.SemaphoreType.DMA((n_bufs,)), pltpu.SemaphoreType.DMA],
        compiler_params=pltpu.CompilerParams(vmem_limit_bytes=VMEM_LIMIT_BYTES),
        name="route",
    )(logits_t)


def _expert_kernel(te_ref, nused_ref, tok_ref, tok_next_ref, xn_ref, wup_ref, bup_ref, wdn_ref, bdn_ref,
                   ys_ref, xs_even, xs_odd, sems):
    i = pl.program_id(0)
    n_used = nused_ref[0]
    tm = MOE_TILE
    bufs = ((xs_even, sems.at[0]), (xs_odd, sems.at[1]))

    def row_copy(t_ref, buf, sem, j):
        src = pl.multiple_of(t_ref[0, 0, j] * SUBLANES, SUBLANES)
        return pltpu.make_async_copy(xn_ref.at[pl.ds(src, SUBLANES), :],
                                     buf.at[pl.ds(j * SUBLANES, SUBLANES), :], sem)

    def wait_tile(buf, sem):
        pltpu.make_async_copy(xn_ref.at[pl.ds(0, tm * SUBLANES), :], buf, sem).wait()

    @pl.when(i == 0)
    def _():
        def body(j, _):
            row_copy(tok_ref, xs_even, sems.at[0], j).start()
            return 0
        lax.fori_loop(0, tm, body, 0, unroll=8)

    def step(cur, nxt):
        buf, sem = cur
        wait_tile(buf, sem)

        def issue(lo, hi):
            for j in range(lo, hi):
                row_copy(tok_next_ref, nxt[0], nxt[1], j).start(priority=j % 2)

        xb = jnp.concatenate([buf[pl.ds(s, tm, stride=SUBLANES), :] for s in range(SUBLANES)],
                             axis=-1).astype(BF16)
        issue(0, tm // 4)
        hu_glu = _dot(xb, wup_ref[0, :, :D_FF].astype(BF16)) + bup_ref[0, :, :D_FF]
        issue(tm // 4, tm // 2)
        hu_lin = _dot(xb, wup_ref[0, :, D_FF:].astype(BF16)) + bup_ref[0, :, D_FF:]
        issue(tm // 2, 3 * tm // 4)
        glu = jnp.minimum(hu_glu, SWIGLU_LIMIT)
        lin = jnp.clip(hu_lin, -SWIGLU_LIMIT, SWIGLU_LIMIT)
        act = glu * jax.nn.sigmoid(SWIGLU_ALPHA * glu) * (lin + 1.0)
        issue(3 * tm // 4, tm)
        ys_ref[...] = _dot(act.astype(BF16), wdn_ref[0].astype(BF16)) + bdn_ref[0]

    for parity in range(2):
        @pl.when(jnp.logical_and(i < n_used, i % 2 == parity))
        def _(parity=parity):
            step(bufs[parity], bufs[1 - parity])

        @pl.when(jnp.logical_and(i == n_used - 1, i % 2 == parity))
        def _(parity=parity):
            wait_tile(*bufs[1 - parity])

    @pl.when(i >= n_used)
    def _():
        ys_ref[...] = jnp.zeros_like(ys_ref)


def _expert_call(tile_e, n_used, row_tok, xn, w_up, b_up, w_down, b_down, n_tiles):
    tm = MOE_TILE

    def w_map(i, te, nu):
        return (te[i], 0, 0)

    grid_spec = pltpu.PrefetchScalarGridSpec(
        num_scalar_prefetch=2,
        grid=(n_tiles,),
        in_specs=[
            pl.BlockSpec((1, 1, tm), lambda i, te, nu: (i, 0, 0), memory_space=pltpu.SMEM),
            pl.BlockSpec((1, 1, tm), lambda i, te, nu: (jnp.clip(i + 1, 0, jnp.maximum(nu[0] - 1, 0)), 0, 0),
                         memory_space=pltpu.SMEM),
            pl.BlockSpec(memory_space=pl.ANY),
            pl.BlockSpec((1, D_MODEL, 2 * D_FF), w_map),
            pl.BlockSpec((1, 1, 2 * D_FF), w_map),
            pl.BlockSpec((1, D_FF, D_MODEL), w_map),
            pl.BlockSpec((1, 1, D_MODEL), w_map),
        ],
        out_specs=pl.BlockSpec((tm, D_MODEL), lambda i, te, nu: (i, 0)),
        scratch_shapes=[pltpu.VMEM((tm * SUBLANES, LANES), F32), pltpu.VMEM((tm * SUBLANES, LANES), F32),
                        pltpu.SemaphoreType.DMA((2,))],
    )
    return pl.pallas_call(
        _expert_kernel,
        grid_spec=grid_spec,
        out_shape=jax.ShapeDtypeStruct((n_tiles * tm, D_MODEL), F32),
        compiler_params=pltpu.CompilerParams(dimension_semantics=("arbitrary",),
                                             vmem_limit_bytes=VMEM_LIMIT_BYTES),
        name="experts",
    )(tile_e, n_used, row_tok.reshape(-1, 1, tm), row_tok.reshape(-1, 1, tm),
      xn, w_up, b_up, w_down, b_down)


def _combine_kernel(dest_ref, dest_next_ref, hmid_ref, gates_ref, gf_ref, ys_ref, yp_ref, ysm_ref,
                    ybuf, res, sems, *, nbp):
    i = pl.program_id(0)
    n = pl.num_programs(0)
    slot = i % 2
    tt = TOKEN_TILE

    def issue(d_ref, s):
        def body(j, _):
            for k in range(TOP_K):
                d = d_ref[k, j]
                pltpu.make_async_copy(ys_ref.at[pl.ds(d, 1), :], ybuf.at[s, k, pl.ds(j, 1), :],
                                      sems.at[s]).start(priority=k % 2)
            return 0
        lax.fori_loop(0, tt, body, 0, unroll=8)

    @pl.when(i == 0)
    def _():
        issue(dest_ref, 0)

    @pl.when(i + 1 < n)
    def _():
        issue(dest_next_ref, 1 - slot)

    for k in range(TOP_K):
        pltpu.make_async_copy(ys_ref.at[pl.ds(0, tt), :], ybuf.at[slot, k], sems.at[slot]).wait()

    g = gates_ref[...]
    g_t = jnp.concatenate([g, jnp.zeros((tt - TOP_K, tt), F32)], axis=0).T
    acc = hmid_ref[...]
    for k in range(TOP_K):
        acc = acc + g_t[:, k:k + 1] * ybuf[slot, k]
    out = _rms(acc, gf_ref[...])
    n_cb = D_MODEL // LANES

    @pl.when(i < n - 1)
    def _():
        for cb in range(n_cb):
            res[cb] = out[:, cb * LANES:(cb + 1) * LANES]
        for b in range(nbp):
            yp_ref[b] = jnp.concatenate(
                [res[cb, pl.ds(b, tt // nbp, stride=nbp), :] for cb in range(n_cb)], axis=-1)

    @pl.when(i == n - 1)
    def _():
        ysm_ref[...] = out


def _combine_call(dest, hmid, gates, gf, ys, nbp, seq, nbs):
    n_tok, _ = hmid.shape
    tt = TOKEN_TILE
    n = n_tok // tt
    assert nbs == tt and n_tok == nbp * seq + nbs and (nbp * seq) % tt == 0
    steps = tt // nbp
    return pl.pallas_call(
        functools.partial(_combine_kernel, nbp=nbp),
        grid=(n,),
        in_specs=[
            pl.BlockSpec((TOP_K, tt), lambda i: (0, i), memory_space=pltpu.SMEM),
            pl.BlockSpec((TOP_K, tt), lambda i: (0, jnp.minimum(i + 1, n - 1)), memory_space=pltpu.SMEM),
            pl.BlockSpec((tt, D_MODEL), lambda i: (i, 0)),
            pl.BlockSpec((TOP_K, tt), lambda i: (0, i)),
            pl.BlockSpec((1, D_MODEL), lambda i: (0, 0)),
            pl.BlockSpec(memory_space=pl.ANY),
        ],
        out_specs=[
            pl.BlockSpec((nbp, steps, D_MODEL), lambda i: (0, jnp.minimum(i, n - 2), 0)),
            pl.BlockSpec((nbs, D_MODEL), lambda i: (0, 0)),
        ],
        out_shape=[jax.ShapeDtypeStruct((nbp, seq, D_MODEL), F32), jax.ShapeDtypeStruct((nbs, D_MODEL), F32)],
        scratch_shapes=[pltpu.VMEM((2, TOP_K, tt, D_MODEL), F32), pltpu.VMEM((D_MODEL // LANES, tt, LANES), F32),
                        pltpu.SemaphoreType.DMA((2,))],
        compiler_params=pltpu.CompilerParams(dimension_semantics=("arbitrary",),
                                             vmem_limit_bytes=VMEM_LIMIT_BYTES),
        name="combine",
    )(dest, dest, hmid, gates, gf, ys)


def _block_diag(blocks):
    n, r, c = blocks.shape
    eye = jnp.eye(n, dtype=blocks.dtype)
    return (blocks[:, :, None, :] * eye[:, None, :, None]).reshape(n * r, n * c)


def _mixer_weights(norm_mix_g, w_in, lam_re, lam_im, log_step, b_re, b_im, c_re, c_im, d, w_glu, b_glu,
                   conv_w, conv_b, w_a, b_a, w_x, b_x, lam, g_s5, g_lru, w_out, g_ffn, w_router, b_router):
    step = jnp.exp(log_step)[:, None]
    dr = lam_re * step
    di = lam_im * step
    mag = jnp.exp(dr)
    lb_re = mag * jnp.cos(di)
    lb_im = mag * jnp.sin(di)
    den = lam_re * lam_re + lam_im * lam_im
    num_re = lb_re - 1.0
    coef_re = (num_re * lam_re + lb_im * lam_im) / den
    coef_im = (lb_im * lam_re - num_re * lam_im) / den
    bb_re = coef_re[..., None] * b_re - coef_im[..., None] * b_im
    bb_im = coef_re[..., None] * b_im + coef_im[..., None] * b_re
    wbu = jnp.concatenate([_block_diag(bb_re.transpose(0, 2, 1)),
                           _block_diag(bb_im.transpose(0, 2, 1))], axis=1).astype(BF16)
    wc = jnp.concatenate([_block_diag(c_re.transpose(0, 2, 1)),
                          -_block_diag(c_im.transpose(0, 2, 1))], axis=0).astype(BF16)
    wax = jnp.concatenate([_block_diag(w_a), _block_diag(w_x)], axis=1).astype(BF16)
    lsl = LRU_C * jax.nn.log_sigmoid(lam)
    wr = jnp.pad(w_router, ((0, 0), (0, LANES - N_EXPERTS)))
    br = jnp.pad(b_router, (0, LANES - N_EXPERTS))
    row = lambda v: v.reshape(1, -1)
    weights = [row(norm_mix_g), w_in.astype(BF16), wbu, row(lb_re), row(lb_im), wc, row(d),
               w_glu.astype(BF16), row(b_glu), conv_w, row(conv_b), wax, row(b_a), row(b_x), row(lsl),
               row(g_s5), row(g_lru), w_out.astype(BF16), row(g_ffn), wr, row(br)]
    assert len(weights) == N_MIXER_WEIGHTS
    return weights


def kernel(x_prompt, x_sample, state_s5_re, state_s5_im, state_lru_h, state_conv, meta_tokens,
           norm_mix_g, w_in, s5_lambda_re, s5_lambda_im, s5_log_step, s5_b_re, s5_b_im, s5_c_re, s5_c_im,
           s5_d, s5_w_glu, s5_b_glu, conv_w, conv_b, lru_w_a, lru_b_a, lru_w_x, lru_b_x, lru_lambda,
           norm_s5_g, norm_lru_g, w_out, norm_ffn_g, w_router, b_router, w_up, b_up, w_down, b_down, norm_f_g):
    bp, seq, _ = x_prompt.shape
    bs = x_sample.shape[0]
    out_dtype = x_prompt.dtype
    n_tok = bp * seq + bs
    assert x_sample.shape[1] == 1 and norm_mix_g.shape[0] == 1

    weights = _mixer_weights(*[a[0].astype(F32) for a in (
        norm_mix_g, w_in, s5_lambda_re, s5_lambda_im, s5_log_step, s5_b_re, s5_b_im, s5_c_re, s5_c_im,
        s5_d, s5_w_glu, s5_b_glu, conv_w, conv_b, lru_w_a, lru_b_a, lru_w_x, lru_b_x, lru_lambda,
        norm_s5_g, norm_lru_g, w_out, norm_ffn_g, w_router, b_router)])

    meta_tm = jnp.broadcast_to(meta_tokens.astype(F32)[:, None, :], (N_META, bp, D_MODEL)).reshape(-1, D_MODEL)
    s5h0 = jnp.concatenate([state_s5_re[0].reshape(bs, S5_CH), state_s5_im[0].reshape(bs, S5_CH)],
                           axis=1).astype(F32)
    conv0 = state_conv[0].astype(F32).transpose(1, 0, 2).reshape((CONV_WIDTH - 1) * bs, LRU_WIDTH)
    (hmid, xn, logits_t, p_s5, p_lru, p_conv, s_s5, s_lru, s_conv) = _mixer_call(
        x_prompt.astype(F32), meta_tm, x_sample.astype(F32).reshape(bs, D_MODEL),
        s5h0, state_lru_h[0].astype(F32), conv0, weights)

    assert n_tok % TOKEN_TILE == 0 and (n_tok * TOP_K) % MOE_TILE == 0
    n_tiles = n_tok * TOP_K // MOE_TILE + N_EXPERTS
    n_tiles_pad = -(-n_tiles // LANES) * LANES
    gates, dest, tile_e, n_used, row_tok = _route_call(logits_t, n_tok, n_tiles, n_tiles_pad)
    ys = _expert_call(tile_e.reshape(-1), n_used.reshape(-1)[:1], row_tok, xn, w_up[0], b_up[0][:, None, :],
                      w_down[0], b_down[0][:, None, :], n_tiles)
    y_prompt, y_sample = _combine_call(dest, hmid, gates, norm_f_g.astype(F32).reshape(1, D_MODEL), ys,
                                       bp, seq, bs)

    def s5_parts(st, b):
        return (st[:, :S5_CH].reshape(1, b, S5_GROUPS, S5_STATE), st[:, S5_CH:].reshape(1, b, S5_GROUPS, S5_STATE))

    def conv_part(cv, b):
        return cv.reshape(CONV_WIDTH - 1, b, LRU_WIDTH).transpose(1, 0, 2)[None]

    p_re, p_im = s5_parts(p_s5, bp)
    s_re, s_im = s5_parts(s_s5, bs)
    return (y_prompt.astype(out_dtype), y_sample.reshape(bs, 1, D_MODEL).astype(out_dtype),
            p_re, p_im, p_lru[None], conv_part(p_conv, bp),
            s_re, s_im, s_lru[None], conv_part(s_conv, bs))
```
